```python
import math
import jax, jax.numpy as jnp
from jax import lax
import numpy as np


D_MODEL = 1024
BATCH = 16
SEQ = 2048
DEPTH = 4

D_ATTN = 512
ATTN_HEADS = 4
ATTN_QK_DIM = 64
ATTN_V_DIM = 128
Q_BLOCK = 128
REL_BUCKETS = 32
REL_MAX_DISTANCE = 128
SUBLN_EPS = 1e-5
D_HYENA = 512
HYENA_ORDER = 2
HYENA_SHORT_CONV = 3
N_DIRS = 2
FILTER_BANDS = 16
FILTER_EMB = 2 * FILTER_BANDS + 1
FILTER_WIDTH = 64
FAST_DECAY_PCT = 0.3
SLOW_DECAY_PCT = 1.5
DECAY_TARGET = 1e-2
NORM_EPS = 1e-6
FILTER_CH = N_DIRS * HYENA_ORDER * D_HYENA
W_IN_COLS = 4 * D_ATTN + 4 * D_HYENA + 2 * D_MODEL

kernel_name = "hybrid_diffattn_hyena_encoder"


def rms_norm(x, gain, eps):
    x32 = x.astype(jnp.float32)
    y = x32 * lax.rsqrt(jnp.mean(x32 * x32, axis=-1, keepdims=True) + eps)
    return (y * gain.astype(jnp.float32)).astype(x.dtype)


def t5_bucket(rel):
    nb = REL_BUCKETS // 2
    max_exact = nb // 2
    ret = jnp.where(rel > 0, nb, 0)
    n = jnp.abs(rel)
    nf = jnp.maximum(n, 1).astype(jnp.float32)
    large = max_exact + (jnp.log(nf / max_exact) / math.log(REL_MAX_DISTANCE / max_exact)
                         * (nb - max_exact)).astype(jnp.int32)
    large = jnp.minimum(large, nb - 1)
    return ret + jnp.where(n < max_exact, n, large)


def diff_attention(q, k, v, lam, rel_bias):
    B, S = q.shape[0], q.shape[1]
    nblk = S // Q_BLOCK
    qb = (q * (ATTN_QK_DIM ** -0.5)).reshape(B, nblk, Q_BLOCK, ATTN_HEADS, 2, ATTN_QK_DIM)
    qb = qb.transpose(1, 0, 3, 4, 2, 5)
    kt = k.transpose(0, 2, 3, 1, 4)
    vt = v.transpose(0, 2, 1, 3)
    k_pos = jnp.arange(S, dtype=jnp.int32)

    def block(args):
        q_blk, i = args
        q_pos = i * Q_BLOCK + jnp.arange(Q_BLOCK, dtype=jnp.int32)
        bucket = t5_bucket(k_pos[None, :] - q_pos[:, None])
        bias = rel_bias[bucket].astype(jnp.float32).transpose(2, 0, 1)
        logits = jnp.einsum('bhcqd,bhckd->bhcqk', q_blk, kt).astype(jnp.float32)
        p = jax.nn.softmax(logits + bias[None, :, None], axis=-1)
        w = p[:, :, 0] - lam * p[:, :, 1]
        return jnp.einsum('bhqk,bhkd->bhqd', w.astype(vt.dtype), vt)

    out = lax.map(block, (qb, jnp.arange(nblk, dtype=jnp.int32)))
    return out.transpose(1, 0, 3, 2, 4).reshape(B, S, ATTN_HEADS, ATTN_V_DIM)


def short_conv3(u, w, b):
    up = jnp.pad(u, ((0, 0), (1, 1), (0, 0)))
    return up[:, :-2] * w[0] + up[:, 1:-1] * w[1] + up[:, 2:] * w[2] + b


def hyena_filters(L, w1, b1, w2, b2, w3, b3, freq, w_out, deltas):
    t = jnp.linspace(0.0, 1.0, L, dtype=jnp.float32)[:, None]
    t_res = jnp.arange(L, dtype=jnp.float32)[:, None]
    f = jnp.linspace(1e-4, FILTER_BANDS - 1, FILTER_BANDS, dtype=jnp.float32)[None, :]
    ang = 2.0 * math.pi * t_res * f / L
    z = jnp.concatenate([t, jnp.cos(ang), -jnp.sin(ang)], axis=-1)
    a = jnp.sin(freq * (z @ w1 + b1))
    a = jnp.sin(freq * (a @ w2 + b2))
    a = jnp.sin(freq * (a @ w3 + b3))
    decay = jnp.exp(-t * jnp.abs(deltas)[None, :])
    h = ((a @ w_out) * decay).astype(jnp.float32).reshape(L, N_DIRS, HYENA_ORDER, D_HYENA)
    fwd, bwd = h[:, 0], h[:, 1]
    two_sided = jnp.concatenate(
        [fwd, jnp.zeros((1, HYENA_ORDER, D_HYENA), jnp.float32), bwd[:0:-1]], axis=0)
    return jnp.fft.rfft(two_sided, axis=0)


def fft_long_conv(u, h_f, bias):
    L = u.shape[1]
    u32 = u.astype(jnp.float32)
    y = jnp.fft.irfft(jnp.fft.rfft(u32, n=2 * L, axis=1) * h_f[None], n=2 * L, axis=1)[:, :L]
    return (y + u32 * bias.astype(jnp.float32)).astype(u.dtype)


def setup_inputs(seed: int = 0) -> dict:
    key = jax.random.key(seed)
    ks = jax.random.split(key, 26)

    def nrm(k, shape, s):
        return jax.random.normal(k, shape, jnp.float32) * s

    deltas0 = jnp.linspace(math.log(FAST_DECAY_PCT) / DECAY_TARGET,
                           math.log(SLOW_DECAY_PCT) / DECAY_TARGET, FILTER_CH, dtype=jnp.float32)
    return {
        "x": nrm(ks[0], (BATCH, SEQ, D_MODEL), 1.0),
        "rel_bias": nrm(ks[1], (REL_BUCKETS, ATTN_HEADS), 0.1),
        "pre_norm": 1.0 + nrm(ks[2], (DEPTH, D_MODEL), 0.02),
        "post_norm": 1.0 + nrm(ks[3], (DEPTH, D_MODEL), 0.02),
        "w_in": nrm(ks[4], (DEPTH, D_MODEL, W_IN_COLS), D_MODEL ** -0.5),
        "lam_q1": nrm(ks[5], (DEPTH, ATTN_QK_DIM), 0.1),
        "lam_k1": nrm(ks[6], (DEPTH, ATTN_QK_DIM), 0.1),
        "lam_q2": nrm(ks[7], (DEPTH, ATTN_QK_DIM), 0.1),
        "lam_k2": nrm(ks[8], (DEPTH, ATTN_QK_DIM), 0.1),
        "subln": 1.0 + nrm(ks[9], (DEPTH, ATTN_V_DIM), 0.02),
        "conv_w": nrm(ks[10], (DEPTH, HYENA_SHORT_CONV, 3 * D_HYENA), HYENA_SHORT_CONV ** -0.5),
        "conv_b": nrm(ks[11], (DEPTH, 3 * D_HYENA), 0.02),
        "flt_w1": nrm(ks[12], (DEPTH, FILTER_EMB, FILTER_WIDTH), FILTER_EMB ** -0.5),
        "flt_b1": nrm(ks[13], (DEPTH, FILTER_WIDTH), 0.1),
        "flt_w2": nrm(ks[14], (DEPTH, FILTER_WIDTH, FILTER_WIDTH), FILTER_WIDTH ** -0.5),
        "flt_b2": nrm(ks[15], (DEPTH, FILTER_WIDTH), 0.1),
        "flt_w3": nrm(ks[16], (DEPTH, FILTER_WIDTH, FILTER_WIDTH), FILTER_WIDTH ** -0.5),
        "flt_b3": nrm(ks[17], (DEPTH, FILTER_WIDTH), 0.1),
        "flt_freq": 1.0 + nrm(ks[18], (DEPTH, FILTER_WIDTH), 0.01),
        "flt_w_out": nrm(ks[19], (DEPTH, FILTER_WIDTH, FILTER_CH), 0.1 * FILTER_WIDTH ** -0.5),
        "flt_deltas": deltas0[None, :] + nrm(ks[20], (DEPTH, FILTER_CH), 0.01),
        "flt_bias": nrm(ks[21], (DEPTH, HYENA_ORDER, D_HYENA), 0.1),
        "w_pa": nrm(ks[22], (DEPTH, D_ATTN, D_MODEL), D_ATTN ** -0.5),
        "w_ph": nrm(ks[23], (DEPTH, D_HYENA, D_MODEL), D_HYENA ** -0.5),
        "w_out": nrm(ks[24], (DEPTH, D_MODEL, D_MODEL), D_MODEL ** -0.5),
    }


def reference(x, rel_bias, pre_norm, post_norm, w_in, lam_q1, lam_k1, lam_q2, lam_k2, subln,
              conv_w, conv_b, flt_w1, flt_b1, flt_w2, flt_b2, flt_w3, flt_b3, flt_freq,
              flt_w_out, flt_deltas, flt_bias, w_pa, w_ph, w_out):
    B, S, _ = x.shape
    widths = (D_ATTN, D_ATTN, D_ATTN, D_ATTN, 3 * D_HYENA, D_HYENA, D_MODEL, D_MODEL)
    splits = [int(c) for c in np.cumsum(widths)[:-1]]
    for l in range(DEPTH):
        lam_init = 0.8 - 0.6 * math.exp(-0.3 * l)
        h = rms_norm(x, pre_norm[l], NORM_EPS)
        proj = h @ w_in[l]
        q, k, v, z_a, u_h, z_h, g_a, g_h = jnp.split(proj, splits, axis=-1)

        lam = (jnp.exp(jnp.sum(lam_q1[l].astype(jnp.float32) * lam_k1[l].astype(jnp.float32)))
               - jnp.exp(jnp.sum(lam_q2[l].astype(jnp.float32) * lam_k2[l].astype(jnp.float32)))
               + lam_init)
        o = diff_attention(q.reshape(B, S, ATTN_HEADS, 2, ATTN_QK_DIM),
                           k.reshape(B, S, ATTN_HEADS, 2, ATTN_QK_DIM),
                           v.reshape(B, S, ATTN_HEADS, ATTN_V_DIM), lam, rel_bias)
        o = rms_norm(o, subln[l], SUBLN_EPS) * (1.0 - lam_init)
        y_a = o.reshape(B, S, D_ATTN) * jax.nn.silu(z_a)

        u_h = short_conv3(u_h, conv_w[l], conv_b[l])
        hv, hx1, hx2 = jnp.split(u_h, 3, axis=-1)
        filt = hyena_filters(S, flt_w1[l], flt_b1[l], flt_w2[l], flt_b2[l], flt_w3[l], flt_b3[l],
                             flt_freq[l], flt_w_out[l], flt_deltas[l])
        z = fft_long_conv(hv, filt[:, 0], flt_bias[l, 0]) * hx1
        z = fft_long_conv(z, filt[:, 1], flt_bias[l, 1]) * hx2
        y_h = z * jax.nn.silu(z_h)

        m = jax.nn.sigmoid(g_a) * (y_a @ w_pa[l]) + jax.nn.sigmoid(g_h) * (y_h @ w_ph[l])
        x = x + rms_norm(m @ w_out[l], post_norm[l], NORM_EPS)
    return x
```

```python
import functools
import math

import numpy as np
import jax
import jax.numpy as jnp
from jax import lax
from jax.experimental import pallas as pl
from jax.experimental.pallas import tpu as pltpu

F32 = jnp.float32
BF16 = jnp.bfloat16

ATTN_HEADS = 4
ATTN_QK_DIM = 64
REL_BUCKETS = 32
REL_MAX_DISTANCE = 128
SUBLN_EPS = 1e-5
NORM_EPS = 1e-6
FILTER_BANDS = 16

LANES = 128
VMEM_LIMIT_BYTES = 56 * 1024 * 1024

ROW_TILE = 512
PROJ_COL_CHUNK = 1024
Q_TILE = 256
CONV_BLOCK = 512
CONV_CH_TILE = 256
SPEC_ROW_CHUNK = 32


def _params(*sem):
    return pltpu.CompilerParams(dimension_semantics=sem, vmem_limit_bytes=VMEM_LIMIT_BYTES)


def _inproj_kernel(x_ref, g_ref, w_ref, o_ref, h_ref, *, col_chunk):
    x = x_ref[...]
    ms = jnp.mean(x * x, axis=-1, keepdims=True)
    h_ref[...] = (x * lax.rsqrt(ms + NORM_EPS) * g_ref[...]).astype(BF16)
    for j in range(o_ref.shape[1] // col_chunk):
        cols = slice(j * col_chunk, (j + 1) * col_chunk)
        o_ref[:, cols] = jnp.dot(h_ref[...], w_ref[:, cols],
                                 preferred_element_type=F32).astype(o_ref.dtype)


def _inproj(x2, gain, w, *, tm, col_chunk):
    m, d = x2.shape
    n = w.shape[1]
    return pl.pallas_call(
        functools.partial(_inproj_kernel, col_chunk=col_chunk),
        out_shape=jax.ShapeDtypeStruct((m, n), BF16),
        grid=(m // tm,),
        in_specs=[
            pl.BlockSpec((tm, d), lambda i: (i, 0)),
            pl.BlockSpec((1, d), lambda i: (0, 0)),
            pl.BlockSpec((d, n), lambda i: (0, 0), pipeline_mode=pl.Buffered(1)),
        ],
        out_specs=pl.BlockSpec((tm, n), lambda i: (i, 0)),
        scratch_shapes=[pltpu.VMEM((tm, d), BF16)],
        compiler_params=_params("parallel"),
        name="inproj",
    )(x2, gain, w)


def _t5_bucket(rel):
    nb = REL_BUCKETS // 2
    max_exact = nb // 2
    ret = jnp.where(rel > 0, nb, 0)
    n = jnp.abs(rel)
    nf = jnp.maximum(n, 1).astype(F32)
    large = max_exact + (jnp.log(nf / max_exact) / math.log(REL_MAX_DISTANCE / max_exact)
                         * (nb - max_exact)).astype(jnp.int32)
    large = jnp.minimum(large, nb - 1)
    return ret + jnp.where(n < max_exact, n, large)


def _bias_kernel(rb_ref, o_ref, *, seq, qb):
    h = pl.program_id(0)
    shape = o_ref.shape
    e = lax.broadcasted_iota(jnp.int32, shape, 0)
    i = lax.broadcasted_iota(jnp.int32, shape, 1)
    c = lax.broadcasted_iota(jnp.int32, shape, 2)
    bucket = _t5_bucket(e * LANES + c - i - (seq - qb))
    acc = jnp.zeros(shape, F32)
    for b in range(REL_BUCKETS):
        acc = jnp.where(bucket == b, rb_ref[b, h], acc)
    o_ref[...] = acc


def _bias_table(rel_bias, *, seq, qb):
    heads = rel_bias.shape[1]
    e = (2 * seq - qb) // LANES
    return pl.pallas_call(
        functools.partial(_bias_kernel, seq=seq, qb=qb),
        out_shape=jax.ShapeDtypeStruct((heads, e, qb, LANES), F32),
        grid=(heads,),
        in_specs=[pl.BlockSpec(memory_space=pltpu.SMEM)],
        out_specs=pl.BlockSpec((None, e, qb, LANES), lambda h: (h, 0, 0, 0)),
        compiler_params=_params("parallel"),
        name="rel_bias_table",
    )(rel_bias.astype(F32))


def _attn_kernel(q_ref, k_ref, v_ref, z_ref, d_ref, lq1_ref, lk1_ref, lq2_ref, lk2_ref,
                 g_ref, o_ref, *, lam_init):
    qb = q_ref.shape[0]
    seq = k_ref.shape[0]
    nkb = seq // LANES
    i = pl.program_id(2)

    lam = (jnp.exp(jnp.sum(lq1_ref[...] * lk1_ref[...], axis=-1, keepdims=True))
           - jnp.exp(jnp.sum(lq2_ref[...] * lk2_ref[...], axis=-1, keepdims=True))
           + lam_init)

    e0 = (seq - qb) // LANES - i * (qb // LANES)
    bias = jnp.concatenate([d_ref[e0 + kb] for kb in range(nkb)], axis=1)

    q = q_ref[...] * (ATTN_QK_DIM ** -0.5)
    k = k_ref[...]
    lane = lax.broadcasted_iota(jnp.int32, q.shape, 1)
    zero = jnp.zeros_like(q)
    contract_last = (((1,), (1,)), ((), ()))
    s0 = lax.dot_general(jnp.where(lane < ATTN_QK_DIM, q, zero), k, contract_last,
                         preferred_element_type=F32) + bias
    s1 = lax.dot_general(jnp.where(lane >= ATTN_QK_DIM, q, zero), k, contract_last,
                         preferred_element_type=F32) + bias
    p0 = jnp.exp(s0 - jnp.max(s0, axis=-1, keepdims=True))
    p1 = jnp.exp(s1 - jnp.max(s1, axis=-1, keepdims=True))
    c0 = 1.0 / jnp.sum(p0, axis=-1, keepdims=True)
    c1 = lam / jnp.sum(p1, axis=-1, keepdims=True)
    w = (p0 * c0 - p1 * c1).astype(BF16)
    o = jnp.dot(w, v_ref[...], preferred_element_type=F32)

    o = o * lax.rsqrt(jnp.mean(o * o, axis=-1, keepdims=True) + SUBLN_EPS) * g_ref[...]
    o = o * (1.0 - lam_init)
    z = z_ref[...].astype(F32)
    o_ref[...] = (o * (z * jax.nn.sigmoid(z))).astype(o_ref.dtype)


def _attention(proj3, dtab, lq1, lk1, lq2, lk2, subln, *, d_attn, lam_init, qb):
    b, s, _ = proj3.shape
    heads = ATTN_HEADS
    dv = d_attn // heads
    assert dv == LANES and 2 * ATTN_QK_DIM == LANES
    e = dtab.shape[1]
    hb = d_attn // dv
    small = pl.BlockSpec((1, ATTN_QK_DIM), lambda bi, h, i: (0, 0))
    return pl.pallas_call(
        functools.partial(_attn_kernel, lam_init=lam_init),
        out_shape=jax.ShapeDtypeStruct((b, s, d_attn), BF16),
        grid=(b, heads, s // qb),
        in_specs=[
            pl.BlockSpec((None, qb, dv), lambda bi, h, i: (bi, i, h)),
            pl.BlockSpec((None, s, dv), lambda bi, h, i: (bi, 0, hb + h)),
            pl.BlockSpec((None, s, dv), lambda bi, h, i: (bi, 0, 2 * hb + h)),
            pl.BlockSpec((None, qb, dv), lambda bi, h, i: (bi, i, 3 * hb + h)),
            pl.BlockSpec((None, e, qb, LANES), lambda bi, h, i: (h, 0, 0, 0)),
            small, small, small, small,
            pl.BlockSpec((1, dv), lambda bi, h, i: (0, 0)),
        ],
        out_specs=pl.BlockSpec((None, qb, dv), lambda bi, h, i: (bi, i, h)),
        compiler_params=_params("parallel", "parallel", "arbitrary"),
        name="diff_attention",
    )(proj3, proj3, proj3, proj3, dtab, lq1, lk1, lq2, lk2, subln)


def _filter_kernel(z_ref, w1_ref, b1_ref, w2_ref, b2_ref, w3_ref, b3_ref, f_ref, wo_ref,
                   dl_ref, o_ref):
    hi = lax.Precision.HIGHEST
    z = z_ref[...]
    f = f_ref[...]
    a = jnp.sin(f * (jnp.dot(z, w1_ref[...], precision=hi, preferred_element_type=F32)
                     + b1_ref[...]))
    a = jnp.sin(f * (jnp.dot(a, w2_ref[...], precision=hi, preferred_element_type=F32)
                     + b2_ref[...]))
    a = jnp.sin(f * (jnp.dot(a, w3_ref[...], precision=hi, preferred_element_type=F32)
                     + b3_ref[...]))
    decay = jnp.exp(-z[:, 0:1] * jnp.abs(dl_ref[...]))
    o_ref[...] = jnp.dot(a, wo_ref[...], precision=hi, preferred_element_type=F32) * decay


def _pad_to(a, rows, cols):
    return jnp.pad(a, ((0, rows - a.shape[0]), (0, cols - a.shape[1])))


def _filters(zfeat, w1, b1, w2, b2, w3, b3, freq, w_out, deltas, *, col_tile=512):
    L = zfeat.shape[0]
    nch = w_out.shape[1]
    row = lambda v: _pad_to(v[None, :], 1, LANES)
    full = lambda shape: pl.BlockSpec(shape, lambda j: (0, 0))
    return pl.pallas_call(
        _filter_kernel,
        out_shape=jax.ShapeDtypeStruct((L, nch), F32),
        grid=(nch // col_tile,),
        in_specs=[
            full((L, LANES)),
            full((LANES, LANES)), full((1, LANES)),
            full((LANES, LANES)), full((1, LANES)),
            full((LANES, LANES)), full((1, LANES)),
            full((1, LANES)),
            pl.BlockSpec((LANES, col_tile), lambda j: (0, j)),
            pl.BlockSpec((1, col_tile), lambda j: (0, j)),
        ],
        out_specs=pl.BlockSpec((L, col_tile), lambda j: (0, j)),
        compiler_params=_params("parallel"),
        name="hyena_filters",
    )(zfeat, _pad_to(w1, LANES, LANES), row(b1), _pad_to(w2, LANES, LANES), row(b2),
      _pad_to(w3, LANES, LANES), row(b3), row(freq), _pad_to(w_out, LANES, nch),
      deltas[None, :])


def _positional_features(L):
    t = jnp.linspace(0.0, 1.0, L, dtype=F32)[:, None]
    t_res = jnp.arange(L, dtype=F32)[:, None]
    f = jnp.linspace(1e-4, FILTER_BANDS - 1, FILTER_BANDS, dtype=F32)[None, :]
    ang = 2.0 * math.pi * t_res * f / L
    z = jnp.concatenate([t, jnp.cos(ang), -jnp.sin(ang)], axis=-1)
    return _pad_to(z, L, LANES)


def _dft_matrices(p):
    k = np.arange(p, dtype=np.int64)[:, None]
    m = np.arange(2 * p, dtype=np.int64)[None, :]
    ang = np.pi * ((k * m) % (2 * p)).astype(np.float64) / p
    cos, sin = np.cos(ang), np.sin(ang)
    fwd_full = np.concatenate([cos, -sin], axis=0)
    fwd_full[p] = np.where(np.arange(2 * p) % 2 == 0, 1.0, -1.0)
    inv = np.concatenate([cos[:, :p].T, -sin[:, :p].T], axis=1) / p
    inv[:, 0] = 0.5 / p
    inv[:, p] = np.where(np.arange(p) % 2 == 0, 0.5, -0.5) / p
    return (fwd_full.astype(np.float32), fwd_full[:, :p].astype(np.float32),
            inv.astype(np.float32))


def _spectrum_kernel(f_ref, g_ref, o_ref):
    p = o_ref.shape[1]
    res = jnp.dot(f_ref[...], g_ref[...], preferred_element_type=F32)
    re, im = res[:p], res[p:]
    first = lax.broadcasted_iota(jnp.int32, re.shape, 0) == 0
    o_ref[0] = re
    o_ref[1] = jnp.where(first, 0.0, im)
    o_ref[2] = jnp.where(first, im, re)


def _filter_spectra(fwd_full, g, *, col_tile):
    two_p, ncols = g.shape
    p = two_p // 2
    return pl.pallas_call(
        _spectrum_kernel,
        out_shape=jax.ShapeDtypeStruct((3, p, ncols), F32),
        grid=(ncols // col_tile,),
        in_specs=[pl.BlockSpec((two_p, two_p), lambda j: (0, 0)),
                  pl.BlockSpec((two_p, col_tile), lambda j: (0, j))],
        out_specs=pl.BlockSpec((3, p, col_tile), lambda j: (0, 0, j)),
        compiler_params=_params("parallel"),
        name="filter_spectra",
    )(fwd_full, g)


def _block_filter_layout(h, *, p, ct):
    L, _, order, c = h.shape
    n = L // p
    fwd, bwd = h[:, 0], h[:, 1]
    zero = jnp.zeros((1, order, c), h.dtype)
    tp = jnp.concatenate([zero, bwd[:0:-1], fwd, zero], axis=0)
    segs = []
    for d in range(-(n - 1), n):
        top = tp[d * p + L: d * p + L + p]
        bot = tp[d * p - p + L: d * p + L]
        bot = jnp.concatenate([zero, bot[1:]], axis=0)
        segs.append(jnp.concatenate([top, bot], axis=0))
    g = jnp.stack(segs, axis=1)
    g = g.reshape(2 * p, 2 * n - 1, order, c // ct, ct).transpose(0, 2, 3, 1, 4)
    return g.reshape(2 * p, -1)


def _short_conv(u, w_ref, b_ref):
    row = lax.broadcasted_iota(jnp.int32, u.shape, 0)
    prev = jnp.where(row == 0, 0.0, pltpu.roll(u, 1, 0))
    nxt = jnp.where(row == u.shape[0] - 1, 0.0, pltpu.roll(u, u.shape[0] - 1, 0))
    return prev * w_ref[0:1, :] + u * w_ref[1:2, :] + nxt * w_ref[2:3, :] + b_ref[...]


def _longconv_kernel(*refs, p, conv_input, final_gate, row_chunk):
    it = iter(refs)
    u_ref = next(it)
    uw_ref, ub_ref = (next(it), next(it)) if conv_input else (None, None)
    x_ref, xw_ref, xb_ref = next(it), next(it), next(it)
    z_ref = next(it) if final_gate else None
    h_ref, bias_ref, fw_ref, inv_ref = next(it), next(it), next(it), next(it)
    o_ref = next(it)
    us_ref, gs_ref, uf_ref, yf_ref = next(it), next(it), next(it), next(it)

    seq, ct = us_ref.shape
    n = seq // p
    u = u_ref[...].astype(F32)
    if conv_input:
        u = _short_conv(u, uw_ref, ub_ref)
    us_ref[...] = u
    gate = _short_conv(x_ref[...].astype(F32), xw_ref, xb_ref)
    if final_gate:
        z = z_ref[...].astype(F32)
        gate = gate * (z * jax.nn.sigmoid(z))
    gs_ref[...] = gate

    for j in range(n):
        uf_ref[j] = jnp.dot(fw_ref[...], us_ref[j * p:(j + 1) * p, :].astype(BF16),
                            preferred_element_type=F32)

    for i in range(n):
        def chunk(r, carry, i=i):
            r0 = pl.multiple_of(r * row_chunk, row_chunk)
            yre = jnp.zeros((row_chunk, ct), F32)
            yim = jnp.zeros((row_chunk, ct), F32)
            for j in range(n):
                cols = slice((i - j + n - 1) * ct, (i - j + n) * ct)
                a = uf_ref[j, pl.ds(r0, row_chunk), :]
                b = uf_ref[j, pl.ds(p + r0, row_chunk), :]
                hre = h_ref[0, pl.ds(r0, row_chunk), cols]
                him = h_ref[1, pl.ds(r0, row_chunk), cols]
                hre2 = h_ref[2, pl.ds(r0, row_chunk), cols]
                yre = yre + (a * hre - b * him)
                yim = yim + (a * him + b * hre2)
            yf_ref[pl.ds(r0, row_chunk), :] = yre.astype(BF16)
            yf_ref[pl.ds(p + r0, row_chunk), :] = yim.astype(BF16)
            return carry
        lax.fori_loop(0, p // row_chunk, chunk, 0)
        y = jnp.dot(inv_ref[...], yf_ref[...], preferred_element_type=F32)
        rows = slice(i * p, (i + 1) * p)
        o_ref[rows, :] = ((y + us_ref[rows, :] * bias_ref[...]) * gs_ref[rows, :]
                          ).astype(o_ref.dtype)


def _longconv(u_arr, u_blk, proj3, gate_blk, z_blk, conv_w, conv_b, hspec, order_idx, bias,
              fwd, inv, *, p, ct, conv_input, final_gate, u_col_blk=None):
    b, s, c_total = u_arr.shape[0], u_arr.shape[1], bias.shape[1]
    nct = c_total // ct
    n = s // p
    lag_cols = (2 * n - 1) * ct

    def sig(col):
        return pl.BlockSpec((None, s, ct), lambda t, bi: (bi, 0, col + t))

    def taps(col):
        return [pl.BlockSpec((3, ct), lambda t, bi: (0, col + t)),
                pl.BlockSpec((1, ct), lambda t, bi: (0, col + t))]

    in_specs, args = [sig(u_blk)], [u_arr]
    if conv_input:
        in_specs += taps(u_blk - u_col_blk)
        args += [conv_w, conv_b]
    in_specs += [sig(gate_blk)] + taps(gate_blk - u_col_blk)
    args += [proj3, conv_w, conv_b]
    if final_gate:
        in_specs.append(sig(z_blk))
        args.append(proj3)
    in_specs += [
        pl.BlockSpec((3, p, lag_cols), lambda t, bi: (0, 0, order_idx * nct + t)),
        pl.BlockSpec((1, ct), lambda t, bi: (0, t)),
        pl.BlockSpec((2 * p, p), lambda t, bi: (0, 0)),
        pl.BlockSpec((p, 2 * p), lambda t, bi: (0, 0)),
    ]
    args += [hspec, bias, fwd, inv]
    return pl.pallas_call(
        functools.partial(_longconv_kernel, p=p, conv_input=conv_input, final_gate=final_gate,
                          row_chunk=min(SPEC_ROW_CHUNK, p)),
        out_shape=jax.ShapeDtypeStruct((b, s, c_total), BF16),
        grid=(nct, b),
        in_specs=in_specs,
        out_specs=pl.BlockSpec((None, s, ct), lambda t, bi: (bi, 0, t)),
        scratch_shapes=[pltpu.VMEM((s, ct), F32), pltpu.VMEM((s, ct), F32),
                        pltpu.VMEM((n, 2 * p, ct), F32), pltpu.VMEM((2 * p, ct), BF16)],
        compiler_params=_params("parallel", "arbitrary"),
        name="hyena_longconv",
    )(*args)


def _merge_kernel(ya_ref, yh_ref, ga_ref, gh_ref, x_ref, wpa_ref, wph_ref, wo_ref, g_ref,
                  o_ref):
    pa = jnp.dot(ya_ref[...], wpa_ref[...], preferred_element_type=F32)
    ph = jnp.dot(yh_ref[...], wph_ref[...], preferred_element_type=F32)
    m = (jax.nn.sigmoid(ga_ref[...].astype(F32)) * pa
         + jax.nn.sigmoid(gh_ref[...].astype(F32)) * ph)
    o = jnp.dot(m.astype(BF16), wo_ref[...], preferred_element_type=F32)
    o = o * lax.rsqrt(jnp.mean(o * o, axis=-1, keepdims=True) + NORM_EPS) * g_ref[...]
    o_ref[...] = x_ref[...] + o


def _merge(ya, yh, proj, x2, wpa, wph, wo, gain, *, tm, gate_blk):
    m, d = x2.shape
    da, dh = ya.shape[1], yh.shape[1]
    const = lambda shape: pl.BlockSpec(shape, lambda i: (0, 0))
    return pl.pallas_call(
        _merge_kernel,
        out_shape=jax.ShapeDtypeStruct((m, d), F32),
        grid=(m // tm,),
        in_specs=[
            pl.BlockSpec((tm, da), lambda i: (i, 0)),
            pl.BlockSpec((tm, dh), lambda i: (i, 0)),
            pl.BlockSpec((tm, d), lambda i: (i, gate_blk)),
            pl.BlockSpec((tm, d), lambda i: (i, gate_blk + 1)),
            pl.BlockSpec((tm, d), lambda i: (i, 0)),
            const((da, d)), const((dh, d)), const((d, d)), const((1, d)),
        ],
        out_specs=pl.BlockSpec((tm, d), lambda i: (i, 0)),
        compiler_params=_params("parallel"),
        name="merge_outproj",
    )(ya, yh, proj, proj, x2, wpa, wph, wo, gain)


def _forward(x, rel_bias, pre_norm, post_norm, w_in, lam_q1, lam_k1, lam_q2, lam_k2, subln,
             conv_w, conv_b, flt_w1, flt_b1, flt_w2, flt_b2, flt_w3, flt_b3, flt_freq,
             flt_w_out, flt_deltas, flt_bias, w_pa, w_ph, w_out, *, tm, qb, p, ct):
    b, s, d = x.shape
    depth = w_in.shape[0]
    d_attn, d_hy = w_pa.shape[1], w_ph.shape[1]
    order = flt_bias.shape[1]
    assert w_in.shape[2] == 4 * d_attn + 4 * d_hy + 2 * d
    assert d_attn % ct == 0 and d_hy % ct == 0 and d % ct == 0

    blk = lambda col: col // ct
    uh0 = 4 * d_attn
    hv_blk, hx1_blk, hx2_blk = blk(uh0), blk(uh0 + d_hy), blk(uh0 + 2 * d_hy)
    zh_blk = blk(uh0 + 3 * d_hy)
    gate_blk = (4 * d_attn + 4 * d_hy) // d

    fwd_full_np, fwd_np, inv_np = _dft_matrices(p)
    fwd_full = jnp.asarray(fwd_full_np).astype(BF16)
    fwd = jnp.asarray(fwd_np).astype(BF16)
    inv = jnp.asarray(inv_np).astype(BF16)
    zfeat = _positional_features(s)
    dtab = _bias_table(rel_bias, seq=s, qb=qb)

    x2 = x.reshape(b * s, d).astype(F32)
    for l in range(depth):
        lam_init = 0.8 - 0.6 * math.exp(-0.3 * l)
        proj = _inproj(x2, pre_norm[l][None, :].astype(F32), w_in[l].astype(BF16),
                       tm=tm, col_chunk=PROJ_COL_CHUNK)
        proj3 = proj.reshape(b, s, -1)

        y_a = _attention(proj3, dtab, lam_q1[l][None, :], lam_k1[l][None, :],
                         lam_q2[l][None, :], lam_k2[l][None, :], subln[l][None, :],
                         d_attn=d_attn, lam_init=lam_init, qb=qb)

        h = _filters(zfeat, flt_w1[l], flt_b1[l], flt_w2[l], flt_b2[l], flt_w3[l], flt_b3[l],
                     flt_freq[l], flt_w_out[l], flt_deltas[l])
        g = _block_filter_layout(h.reshape(s, 2, order, d_hy), p=p, ct=ct).astype(BF16)
        hspec = _filter_spectra(fwd_full, g, col_tile=ct)
        cw, cb = conv_w[l], conv_b[l][None, :]
        z1 = _longconv(proj3, hv_blk, proj3, hx1_blk, None, cw, cb, hspec, 0,
                       flt_bias[l, 0][None, :], fwd, inv, p=p, ct=ct,
                       conv_input=True, final_gate=False, u_col_blk=hv_blk)
        y_h = _longconv(z1, 0, proj3, hx2_blk, zh_blk, cw, cb, hspec, 1,
                        flt_bias[l, 1][None, :], fwd, inv, p=p, ct=ct,
                        conv_input=False, final_gate=True, u_col_blk=hv_blk)

        x2 = _merge(y_a.reshape(b * s, d_attn), y_h.reshape(b * s, d_hy), proj, x2,
                    w_pa[l].astype(BF16), w_ph[l].astype(BF16), w_out[l].astype(BF16),
                    post_norm[l][None, :].astype(F32), tm=tm, gate_blk=gate_blk)
    return x2.reshape(b, s, d).astype(x.dtype)


def kernel(x, rel_bias, pre_norm, post_norm, w_in, lam_q1, lam_k1, lam_q2, lam_k2, subln, conv_w, conv_b, flt_w1, flt_b1, flt_w2, flt_b2, flt_w3, flt_b3, flt_freq, flt_w_out, flt_deltas, flt_bias, w_pa, w_ph, w_out):
    b, s, _ = x.shape
    return _forward(x, rel_bias, pre_norm, post_norm, w_in, lam_q1, lam_k1, lam_q2, lam_k2,
                    subln, conv_w, conv_b, flt_w1, flt_b1, flt_w2, flt_b2, flt_w3, flt_b3,
                    flt_freq, flt_w_out, flt_deltas, flt_bias, w_pa, w_ph, w_out,
                    tm=min(ROW_TILE, b * s), qb=min(Q_TILE, s), p=min(CONV_BLOCK, s),
                    ct=CONV_CH_TILE)
```

```python
import functools
import math

import numpy as np
import jax
import jax.numpy as jnp
from jax import lax
from jax.experimental import pallas as pl
from jax.experimental.pallas import tpu as pltpu

F32 = jnp.float32
BF16 = jnp.bfloat16

ATTN_HEADS = 4
ATTN_QK_DIM = 64
REL_BUCKETS = 32
REL_MAX_DISTANCE = 128
SUBLN_EPS = 1e-5
NORM_EPS = 1e-6
FILTER_BANDS = 16

LANES = 128
VMEM_LIMIT_BYTES = 56 * 1024 * 1024

ROW_TILE = 512
PROJ_COL_CHUNK = 1024
Q_TILE = 256
CONV_BLOCK = 512
CONV_CH_TILE = 256
SPEC_ROW_CHUNK = 32


def _params(*sem):
    return pltpu.CompilerParams(dimension_semantics=sem, vmem_limit_bytes=VMEM_LIMIT_BYTES)


def _inproj_kernel(x_ref, g_ref, w_ref, o_ref, h_ref, *, col_chunk):
    x = x_ref[...]
    ms = jnp.mean(x * x, axis=-1, keepdims=True)
    h_ref[...] = (x * lax.rsqrt(ms + NORM_EPS) * g_ref[...]).astype(BF16)
    for j in range(o_ref.shape[1] // col_chunk):
        cols = slice(j * col_chunk, (j + 1) * col_chunk)
        o_ref[:, cols] = jnp.dot(h_ref[...], w_ref[:, cols],
                                 preferred_element_type=F32).astype(o_ref.dtype)


def _inproj(x2, gain, w, *, tm, col_chunk):
    m, d = x2.shape
    n = w.shape[1]
    return pl.pallas_call(
        functools.partial(_inproj_kernel, col_chunk=col_chunk),
        out_shape=jax.ShapeDtypeStruct((m, n), BF16),
        grid=(m // tm,),
        in_specs=[
            pl.BlockSpec((tm, d), lambda i: (i, 0)),
            pl.BlockSpec((1, d), lambda i: (0, 0)),
            pl.BlockSpec((d, n), lambda i: (0, 0), pipeline_mode=pl.Buffered(1)),
        ],
        out_specs=pl.BlockSpec((tm, n), lambda i: (i, 0)),
        scratch_shapes=[pltpu.VMEM((tm, d), BF16)],
        compiler_params=_params("parallel"),
        name="inproj",
    )(x2, gain, w)


def _t5_bucket(rel):
    nb = REL_BUCKETS // 2
    max_exact = nb // 2
    ret = jnp.where(rel > 0, nb, 0)
    n = jnp.abs(rel)
    nf = jnp.maximum(n, 1).astype(F32)
    large = max_exact + (jnp.log(nf / max_exact) / math.log(REL_MAX_DISTANCE / max_exact)
                         * (nb - max_exact)).astype(jnp.int32)
    large = jnp.minimum(large, nb - 1)
    return ret + jnp.where(n < max_exact, n, large)


def _bias_kernel(rb_ref, o_ref, *, seq, qb):
    h = pl.program_id(0)
    shape = o_ref.shape
    e = lax.broadcasted_iota(jnp.int32, shape, 0)
    i = lax.broadcasted_iota(jnp.int32, shape, 1)
    c = lax.broadcasted_iota(jnp.int32, shape, 2)
    bucket = _t5_bucket(e * LANES + c - i - (seq - qb))
    acc = jnp.zeros(shape, F32)
    for b in range(REL_BUCKETS):
        acc = jnp.where(bucket == b, rb_ref[b, h], acc)
    o_ref[...] = acc


def _bias_table(rel_bias, *, seq, qb):
    heads = rel_bias.shape[1]
    e = (2 * seq - qb) // LANES
    return pl.pallas_call(
        functools.partial(_bias_kernel, seq=seq, qb=qb),
        out_shape=jax.ShapeDtypeStruct((heads, e, qb, LANES), F32),
        grid=(heads,),
        in_specs=[pl.BlockSpec(memory_space=pltpu.SMEM)],
        out_specs=pl.BlockSpec((None, e, qb, LANES), lambda h: (h, 0, 0, 0)),
        compiler_params=_params("parallel"),
        name="rel_bias_table",
    )(rel_bias.astype(F32))


def _attn_kernel(q_ref, k_ref, v_ref, z_ref, d_ref, lq1_ref, lk1_ref, lq2_ref, lk2_ref,
                 g_ref, o_ref, *, lam_init):
    qb = q_ref.shape[0]
    seq = k_ref.shape[0]
    nkb = seq // LANES
    i = pl.program_id(2)

    lam = (jnp.exp(jnp.sum(lq1_ref[...] * lk1_ref[...], axis=-1, keepdims=True))
           - jnp.exp(jnp.sum(lq2_ref[...] * lk2_ref[...], axis=-1, keepdims=True))
           + lam_init)

    e0 = (seq - qb) // LANES - i * (qb // LANES)
    bias = jnp.concatenate([d_ref[e0 + kb] for kb in range(nkb)], axis=1)

    q = q_ref[...] * (ATTN_QK_DIM ** -0.5)
    k = k_ref[...]
    lane = lax.broadcasted_iota(jnp.int32, q.shape, 1)
    zero = jnp.zeros_like(q)
    contract_last = (((1,), (1,)), ((), ()))
    s0 = lax.dot_general(jnp.where(lane < ATTN_QK_DIM, q, zero), k, contract_last,
                         preferred_element_type=F32) + bias
    s1 = lax.dot_general(jnp.where(lane >= ATTN_QK_DIM, q, zero), k, contract_last,
                         preferred_element_type=F32) + bias
    p0 = jnp.exp(s0 - jnp.max(s0, axis=-1, keepdims=True))
    p1 = jnp.exp(s1 - jnp.max(s1, axis=-1, keepdims=True))
    c0 = 1.0 / jnp.sum(p0, axis=-1, keepdims=True)
    c1 = lam / jnp.sum(p1, axis=-1, keepdims=True)
    w = (p0 * c0 - p1 * c1).astype(BF16)
    o = jnp.dot(w, v_ref[...], preferred_element_type=F32)

    o = o * lax.rsqrt(jnp.mean(o * o, axis=-1, keepdims=True) + SUBLN_EPS) * g_ref[...]
    o = o * (1.0 - lam_init)
    z = z_ref[...].astype(F32)
    o_ref[...] = (o * (z * jax.nn.sigmoid(z))).astype(o_ref.dtype)


def _attention(proj3, dtab, lq1, lk1, lq2, lk2, subln, *, d_attn, lam_init, qb):
    b, s, _ = proj3.shape
    heads = ATTN_HEADS
    dv = d_attn // heads
    assert dv == LANES and 2 * ATTN_QK_DIM == LANES
    e = dtab.shape[1]
    hb = d_attn // dv
    small = pl.BlockSpec((1, ATTN_QK_DIM), lambda bi, h, i: (0, 0))
    return pl.pallas_call(
        functools.partial(_attn_kernel, lam_init=lam_init),
        out_shape=jax.ShapeDtypeStruct((b, s, d_attn), BF16),
        grid=(b, heads, s // qb),
        in_specs=[
            pl.BlockSpec((None, qb, dv), lambda bi, h, i: (bi, i, h)),
            pl.BlockSpec((None, s, dv), lambda bi, h, i: (bi, 0, hb + h)),
            pl.BlockSpec((None, s, dv), lambda bi, h, i: (bi, 0, 2 * hb + h)),
            pl.BlockSpec((None, qb, dv), lambda bi, h, i: (bi, i, 3 * hb + h)),
            pl.BlockSpec((None, e, qb, LANES), lambda bi, h, i: (h, 0, 0, 0)),
            small, small, small, small,
            pl.BlockSpec((1, dv), lambda bi, h, i: (0, 0)),
        ],
        out_specs=pl.BlockSpec((None, qb, dv), lambda bi, h, i: (bi, i, h)),
        compiler_params=_params("parallel", "parallel", "arbitrary"),
        name="diff_attention",
    )(proj3, proj3, proj3, proj3, dtab, lq1, lk1, lq2, lk2, subln)


def _dft_matrices(p):
    k = np.arange(p, dtype=np.int64)[:, None]
    m = np.arange(2 * p, dtype=np.int64)[None, :]
    ang = np.pi * ((k * m) % (2 * p)).astype(np.float64) / p
    cos, sin = np.cos(ang), np.sin(ang)
    fwd_full = np.concatenate([cos, -sin], axis=0)
    fwd_full[p] = np.where(np.arange(2 * p) % 2 == 0, 1.0, -1.0)
    inv = np.concatenate([cos[:, :p].T, -sin[:, :p].T], axis=1) / p
    inv[:, 0] = 0.5 / p
    inv[:, p] = np.where(np.arange(p) % 2 == 0, 0.5, -0.5) / p
    bottom = fwd_full[:, p:].copy()
    bottom[:, 0] = 0.0
    pair = np.concatenate([bottom, fwd_full[:, :p]], axis=1)
    return (fwd_full[:, :p].astype(np.float32), inv.astype(np.float32),
            pair.astype(np.float32))


def _positional_features(L):
    t = jnp.linspace(0.0, 1.0, L, dtype=F32)[:, None]
    t_res = jnp.arange(L, dtype=F32)[:, None]
    f = jnp.linspace(1e-4, FILTER_BANDS - 1, FILTER_BANDS, dtype=F32)[None, :]
    ang = 2.0 * math.pi * t_res * f / L
    z = jnp.concatenate([t, jnp.cos(ang), -jnp.sin(ang)], axis=-1)
    return jnp.pad(z, ((0, 0), (0, LANES - z.shape[1])))


def _filter_spectra_kernel(z_ref, w1_ref, b1_ref, w2_ref, b2_ref, w3_ref, b3_ref, f_ref,
                           wof_ref, wob_ref, dlf_ref, dlb_ref, ftop_ref, fpair_ref, o_ref,
                           a_ref, hf_ref, hb_ref, *, p):
    hi = lax.Precision.HIGHEST
    L, ct = hf_ref.shape
    n = L // p

    @pl.when(pl.program_id(1) == 0)
    def _():
        f = f_ref[...]
        a = jnp.sin(f * (jnp.dot(z_ref[...], w1_ref[...], precision=hi,
                                 preferred_element_type=F32) + b1_ref[...]))
        a = jnp.sin(f * (jnp.dot(a, w2_ref[...], precision=hi,
                                 preferred_element_type=F32) + b2_ref[...]))
        a_ref[...] = jnp.sin(f * (jnp.dot(a, w3_ref[...], precision=hi,
                                          preferred_element_type=F32) + b3_ref[...]))

    a = a_ref[...]
    t = z_ref[:, 0:1]
    hf = (jnp.dot(a, wof_ref[...], precision=hi, preferred_element_type=F32)
          * jnp.exp(-t * jnp.abs(dlf_ref[...])))
    hb = (jnp.dot(a, wob_ref[...], precision=hi, preferred_element_type=F32)
          * jnp.exp(-t * jnp.abs(dlb_ref[...])))
    hf_ref[...] = hf.astype(BF16)
    hb_ref[...] = jnp.where(lax.broadcasted_iota(jnp.int32, hb.shape, 0) == 0, 0.0, hb
                            ).astype(BF16)

    def spectrum(x_ref, e):
        if e == 0:
            return jnp.dot(ftop_ref[...], x_ref[0:p, :], preferred_element_type=F32)
        return jnp.dot(fpair_ref[...], x_ref[(e - 1) * p:(e + 1) * p, :],
                       preferred_element_type=F32)

    first = lax.broadcasted_iota(jnp.int32, (p, ct), 0) == 0

    def conj_im(im):
        return jnp.where(first, im, -im)

    for dd in range(2 * n - 1):
        d = dd - (n - 1)
        if d > 0:
            s = spectrum(hf_ref, d)
            re, im = s[:p], s[p:]
        elif d < 0:
            s = spectrum(hb_ref, -d)
            re, im = s[:p], conj_im(s[p:])
        else:
            sf, sb = spectrum(hf_ref, 0), spectrum(hb_ref, 0)
            re, im = sf[:p] + sb[:p], sf[p:] + conj_im(sb[p:])
        cols = slice(dd * ct, (dd + 1) * ct)
        o_ref[0, :, cols] = re
        o_ref[1, :, cols] = jnp.where(first, 0.0, im)
        o_ref[2, :, cols] = jnp.where(first, im, re)


def _filter_spectra(zfeat, w1, b1, w2, b2, w3, b3, freq, w_out, deltas, ftop, fpair, *,
                    p, ct, order):
    depth = w1.shape[0]
    L = zfeat.shape[0]
    nch = w_out.shape[2]
    ntile = nch // 2 // ct
    assert ntile * ct * 2 == nch and ntile % order == 0
    n = L // p
    lag_cols = (2 * n - 1) * ct

    def pad(a, rows, cols):
        return jnp.pad(a.astype(F32), ((0, 0), (0, rows - a.shape[1]), (0, cols - a.shape[2])))

    mat = pl.BlockSpec((None, LANES, LANES), lambda l, t: (l, 0, 0))
    vec = pl.BlockSpec((None, 1, LANES), lambda l, t: (l, 0, 0))
    const = lambda shape: pl.BlockSpec(shape, lambda l, t: (0, 0))
    w_out_p = pad(w_out, LANES, nch)
    deltas3 = deltas.astype(F32)[:, None, :]
    return pl.pallas_call(
        functools.partial(_filter_spectra_kernel, p=p),
        out_shape=jax.ShapeDtypeStruct((depth, 3, p, ntile * lag_cols), F32),
        grid=(depth, ntile),
        in_specs=[
            const((L, LANES)),
            mat, vec, mat, vec, mat, vec, vec,
            pl.BlockSpec((None, LANES, ct), lambda l, t: (l, 0, t)),
            pl.BlockSpec((None, LANES, ct), lambda l, t: (l, 0, ntile + t)),
            pl.BlockSpec((None, 1, ct), lambda l, t: (l, 0, t)),
            pl.BlockSpec((None, 1, ct), lambda l, t: (l, 0, ntile + t)),
            const((2 * p, p)), const((2 * p, 2 * p)),
        ],
        out_specs=pl.BlockSpec((None, 3, p, lag_cols), lambda l, t: (l, 0, 0, t)),
        scratch_shapes=[pltpu.VMEM((L, LANES), F32), pltpu.VMEM((L, ct), BF16),
                        pltpu.VMEM((L, ct), BF16)],
        compiler_params=_params("parallel", "arbitrary"),
        name="filter_spectra",
    )(zfeat, pad(w1, LANES, LANES), pad(b1[:, None, :], 1, LANES),
      pad(w2, LANES, LANES), pad(b2[:, None, :], 1, LANES),
      pad(w3, LANES, LANES), pad(b3[:, None, :], 1, LANES), pad(freq[:, None, :], 1, LANES),
      w_out_p, w_out_p, deltas3, deltas3, ftop, fpair)


def _short_conv(u, w_ref, b_ref):
    row = lax.broadcasted_iota(jnp.int32, u.shape, 0)
    prev = jnp.where(row == 0, 0.0, pltpu.roll(u, 1, 0))
    nxt = jnp.where(row == u.shape[0] - 1, 0.0, pltpu.roll(u, u.shape[0] - 1, 0))
    return prev * w_ref[0:1, :] + u * w_ref[1:2, :] + nxt * w_ref[2:3, :] + b_ref[...]


def _longconv_kernel(*refs, p, conv_input, final_gate, row_chunk):
    it = iter(refs)
    u_ref = next(it)
    uw_ref, ub_ref = (next(it), next(it)) if conv_input else (None, None)
    x_ref, xw_ref, xb_ref = next(it), next(it), next(it)
    z_ref = next(it) if final_gate else None
    h_ref, bias_ref, fw_ref, inv_ref = next(it), next(it), next(it), next(it)
    o_ref = next(it)
    us_ref, gs_ref, uf_ref, yf_ref = next(it), next(it), next(it), next(it)

    seq, ct = us_ref.shape
    n = seq // p
    u = u_ref[...].astype(F32)
    if conv_input:
        u = _short_conv(u, uw_ref, ub_ref)
    us_ref[...] = u
    gate = _short_conv(x_ref[...].astype(F32), xw_ref, xb_ref)
    if final_gate:
        z = z_ref[...].astype(F32)
        gate = gate * (z * jax.nn.sigmoid(z))
    gs_ref[...] = gate

    for j in range(n):
        uf_ref[j] = jnp.dot(fw_ref[...], us_ref[j * p:(j + 1) * p, :].astype(BF16),
                            preferred_element_type=F32)

    for i in range(n):
        for r0 in range(0, p, row_chunk):
            lo, hi = slice(r0, r0 + row_chunk), slice(p + r0, p + r0 + row_chunk)
            yre = jnp.zeros((row_chunk, ct), F32)
            yim = jnp.zeros((row_chunk, ct), F32)
            for j in range(n):
                cols = slice((i - j + n - 1) * ct, (i - j + n) * ct)
                a, b = uf_ref[j, lo, :], uf_ref[j, hi, :]
                hre, him, hre2 = h_ref[0, lo, cols], h_ref[1, lo, cols], h_ref[2, lo, cols]
                yre = yre + (a * hre - b * him)
                yim = yim + (a * him + b * hre2)
            yf_ref[i, lo, :] = yre.astype(BF16)
            yf_ref[i, hi, :] = yim.astype(BF16)
        y = jnp.dot(inv_ref[...], yf_ref[i], preferred_element_type=F32)
        rows = slice(i * p, (i + 1) * p)
        o_ref[rows, :] = ((y + us_ref[rows, :] * bias_ref[...]) * gs_ref[rows, :]
                          ).astype(o_ref.dtype)


def _longconv(u_arr, u_blk, proj3, gate_blk, z_blk, conv_w, conv_b, hspec, layer, order_idx,
              bias, fwd, inv, *, p, ct, conv_input, final_gate, u_col_blk=None):
    b, s, c_total = u_arr.shape[0], u_arr.shape[1], bias.shape[1]
    nct = c_total // ct
    n = s // p
    lag_cols = (2 * n - 1) * ct

    def sig(col):
        return pl.BlockSpec((None, s, ct), lambda t, bi: (bi, 0, col + t))

    def taps(col):
        return [pl.BlockSpec((3, ct), lambda t, bi: (0, col + t)),
                pl.BlockSpec((1, ct), lambda t, bi: (0, col + t))]

    in_specs, args = [sig(u_blk)], [u_arr]
    if conv_input:
        in_specs += taps(u_blk - u_col_blk)
        args += [conv_w, conv_b]
    in_specs += [sig(gate_blk)] + taps(gate_blk - u_col_blk)
    args += [proj3, conv_w, conv_b]
    if final_gate:
        in_specs.append(sig(z_blk))
        args.append(proj3)
    in_specs += [
        pl.BlockSpec((None, 3, p, lag_cols), lambda t, bi: (layer, 0, 0, order_idx * nct + t)),
        pl.BlockSpec((1, ct), lambda t, bi: (0, t)),
        pl.BlockSpec((2 * p, p), lambda t, bi: (0, 0)),
        pl.BlockSpec((p, 2 * p), lambda t, bi: (0, 0)),
    ]
    args += [hspec, bias, fwd, inv]
    return pl.pallas_call(
        functools.partial(_longconv_kernel, p=p, conv_input=conv_input, final_gate=final_gate,
                          row_chunk=min(SPEC_ROW_CHUNK, p)),
        out_shape=jax.ShapeDtypeStruct((b, s, c_total), BF16),
        grid=(nct, b),
        in_specs=in_specs,
        out_specs=pl.BlockSpec((None, s, ct), lambda t, bi: (bi, 0, t)),
        scratch_shapes=[pltpu.VMEM((s, ct), F32), pltpu.VMEM((s, ct), F32),
                        pltpu.VMEM((n, 2 * p, ct), F32), pltpu.VMEM((n, 2 * p, ct), BF16)],
        compiler_params=_params("parallel", "arbitrary"),
        name="hyena_longconv",
    )(*args)


def _merge_kernel(ya_ref, yh_ref, ga_ref, gh_ref, x_ref, wpa_ref, wph_ref, wo_ref, g_ref,
                  o_ref):
    pa = jnp.dot(ya_ref[...], wpa_ref[...], preferred_element_type=F32)
    ph = jnp.dot(yh_ref[...], wph_ref[...], preferred_element_type=F32)
    m = (jax.nn.sigmoid(ga_ref[...].astype(F32)) * pa
         + jax.nn.sigmoid(gh_ref[...].astype(F32)) * ph)
    o = jnp.dot(m.astype(BF16), wo_ref[...], preferred_element_type=F32)
    o = o * lax.rsqrt(jnp.mean(o * o, axis=-1, keepdims=True) + NORM_EPS) * g_ref[...]
    o_ref[...] = x_ref[...] + o


def _merge(ya, yh, proj, x2, wpa, wph, wo, gain, *, tm, gate_blk):
    m, d = x2.shape
    da, dh = ya.shape[1], yh.shape[1]
    const = lambda shape: pl.BlockSpec(shape, lambda i: (0, 0))
    return pl.pallas_call(
        _merge_kernel,
        out_shape=jax.ShapeDtypeStruct((m, d), F32),
        grid=(m // tm,),
        in_specs=[
            pl.BlockSpec((tm, da), lambda i: (i, 0)),
            pl.BlockSpec((tm, dh), lambda i: (i, 0)),
            pl.BlockSpec((tm, d), lambda i: (i, gate_blk)),
            pl.BlockSpec((tm, d), lambda i: (i, gate_blk + 1)),
            pl.BlockSpec((tm, d), lambda i: (i, 0)),
            const((da, d)), const((dh, d)), const((d, d)), const((1, d)),
        ],
        out_specs=pl.BlockSpec((tm, d), lambda i: (i, 0)),
        compiler_params=_params("parallel"),
        name="merge_outproj",
    )(ya, yh, proj, proj, x2, wpa, wph, wo, gain)


def _forward(x, rel_bias, pre_norm, post_norm, w_in, lam_q1, lam_k1, lam_q2, lam_k2, subln,
             conv_w, conv_b, flt_w1, flt_b1, flt_w2, flt_b2, flt_w3, flt_b3, flt_freq,
             flt_w_out, flt_deltas, flt_bias, w_pa, w_ph, w_out, *, tm, qb, p, ct):
    b, s, d = x.shape
    depth = w_in.shape[0]
    d_attn, d_hy = w_pa.shape[1], w_ph.shape[1]
    order = flt_bias.shape[1]
    assert w_in.shape[2] == 4 * d_attn + 4 * d_hy + 2 * d
    assert d_attn % ct == 0 and d_hy % ct == 0 and d % ct == 0

    blk = lambda col: col // ct
    uh0 = 4 * d_attn
    hv_blk, hx1_blk, hx2_blk = blk(uh0), blk(uh0 + d_hy), blk(uh0 + 2 * d_hy)
    zh_blk = blk(uh0 + 3 * d_hy)
    gate_blk = (4 * d_attn + 4 * d_hy) // d

    fwd_np, inv_np, pair_np = _dft_matrices(p)
    fwd = jnp.asarray(fwd_np).astype(BF16)
    inv = jnp.asarray(inv_np).astype(BF16)
    pair = jnp.asarray(pair_np).astype(BF16)
    dtab = _bias_table(rel_bias, seq=s, qb=qb)
    hspec = _filter_spectra(_positional_features(s), flt_w1, flt_b1, flt_w2, flt_b2, flt_w3,
                            flt_b3, flt_freq, flt_w_out, flt_deltas, fwd, pair,
                            p=p, ct=ct, order=order)

    x2 = x.reshape(b * s, d).astype(F32)
    for l in range(depth):
        lam_init = 0.8 - 0.6 * math.exp(-0.3 * l)
        proj = _inproj(x2, pre_norm[l][None, :].astype(F32), w_in[l].astype(BF16),
                       tm=tm, col_chunk=PROJ_COL_CHUNK)
        proj3 = proj.reshape(b, s, -1)

        y_a = _attention(proj3, dtab, lam_q1[l][None, :], lam_k1[l][None, :],
                         lam_q2[l][None, :], lam_k2[l][None, :], subln[l][None, :],
                         d_attn=d_attn, lam_init=lam_init, qb=qb)

        cw, cb = conv_w[l], conv_b[l][None, :]
        z1 = _longconv(proj3, hv_blk, proj3, hx1_blk, None, cw, cb, hspec, l, 0,
                       flt_bias[l, 0][None, :], fwd, inv, p=p, ct=ct,
                       conv_input=True, final_gate=False, u_col_blk=hv_blk)
        y_h = _longconv(z1, 0, proj3, hx2_blk, zh_blk, cw, cb, hspec, l, 1,
                        flt_bias[l, 1][None, :], fwd, inv, p=p, ct=ct,
                        conv_input=False, final_gate=True, u_col_blk=hv_blk)

        x2 = _merge(y_a.reshape(b * s, d_attn), y_h.reshape(b * s, d_hy), proj, x2,
                    w_pa[l].astype(BF16), w_ph[l].astype(BF16), w_out[l].astype(BF16),
                    post_norm[l][None, :].astype(F32), tm=tm, gate_blk=gate_blk)
    return x2.reshape(b, s, d).astype(x.dtype)


def kernel(x, rel_bias, pre_norm, post_norm, w_in, lam_q1, lam_k1, lam_q2, lam_k2, subln, conv_w, conv_b, flt_w1, flt_b1, flt_w2, flt_b2, flt_w3, flt_b3, flt_freq, flt_w_out, flt_deltas, flt_bias, w_pa, w_ph, w_out):
    b, s, _ = x.shape
    return _forward(x, rel_bias, pre_norm, post_norm, w_in, lam_q1, lam_k1, lam_q2, lam_k2,
                    subln, conv_w, conv_b, flt_w1, flt_b1, flt_w2, flt_b2, flt_w3, flt_b3,
                    flt_freq, flt_w_out, flt_deltas, flt_bias, w_pa, w_ph, w_out,
                    tm=min(ROW_TILE, b * s), qb=min(Q_TILE, s), p=min(CONV_BLOCK, s),
                    ct=CONV_CH_TILE)
```
